```python
import math
import jax
import jax.numpy as jnp
from jax import lax
import numpy as np

D_MODEL = 1024
BATCH = 8
SEQ = 4096
DEPTH = 4
DEC_BATCH = 8
DEC_SEQ = 64
PAST_LEN = 2048

CHUNK = 64
N_HEADS = 4
HEAD_DIM = 64
V_DIM = 2 * HEAD_DIM
ATTN_WIDTH = N_HEADS * V_DIM
QK_COLS = N_HEADS * 2 * HEAD_DIM
ROT_DIM = HEAD_DIM // 4
ROPE_THETA = 500000.0
CONV_CH = D_MODEL - ATTN_WIDTH
CONV_WIDTH = 31
CONV_STATE = CONV_WIDTH - 1
MIX_WIDTH = ATTN_WIDTH + CONV_CH
IN_COLS = 2 * QK_COLS + ATTN_WIDTH + 2 * CONV_CH
N_EXPERTS = 16
N_GROUPS = 4
EXPERTS_PER_GROUP = N_EXPERTS // N_GROUPS
TOP_K = 2
D_EXPERT = D_MODEL // 2
DEEPNORM_ALPHA = (2 * DEPTH) ** 0.25
DEEPNORM_BETA = (8 * DEPTH) ** -0.25
Q_BLOCK = 128
LN_EPS = 1e-5

kernel_name = 'hymba_diffattn_conformer_conv_grouped_moe_stream'


def _layer_norm(x, g, b, eps=LN_EPS):
    xf = x.astype(jnp.float32)
    mu = jnp.mean(xf, -1, keepdims=True)
    var = jnp.mean(jnp.square(xf - mu), -1, keepdims=True)
    y = (xf - mu) * lax.rsqrt(var + eps) * g.astype(jnp.float32) + b.astype(jnp.float32)
    return y.astype(x.dtype)


def _rms_norm(x, g, eps=LN_EPS):
    xf = x.astype(jnp.float32)
    y = xf * lax.rsqrt(jnp.mean(xf * xf, -1, keepdims=True) + eps) * g.astype(jnp.float32)
    return y.astype(x.dtype)


def _partial_rope(x, pos):
    half = ROT_DIM // 2
    inv_freq = jnp.power(ROPE_THETA, -jnp.arange(half, dtype=jnp.float32) * 2.0 / ROT_DIM)
    ang = pos.astype(jnp.float32)[:, None] * inv_freq[None, :]
    cos = jnp.cos(ang)[None, :, None, None, :]
    sin = jnp.sin(ang)[None, :, None, None, :]
    xr = x[..., :ROT_DIM].astype(jnp.float32)
    x1, x2 = xr[..., :half], xr[..., half:]
    rot = jnp.concatenate([x1 * cos - x2 * sin, x2 * cos + x1 * sin], -1).astype(x.dtype)
    return jnp.concatenate([rot, x[..., ROT_DIM:]], -1)


def _diff_attend(q, k, v, lam, mask):
    s = jnp.einsum('bqhmd,bkhmd->bhmqk', q, k).astype(jnp.float32) * (HEAD_DIM ** -0.5)
    if mask is not None:
        s = jnp.where(mask, s, -jnp.inf)
    p = jax.nn.softmax(s, axis=-1)
    a = p[:, :, 0] - lam * p[:, :, 1]
    return jnp.einsum('bhqk,bkhe->bqhe', a.astype(v.dtype), v)


def _prompt_attention(q, k, v, lam):
    B, S = q.shape[0], q.shape[1]
    nb = S // Q_BLOCK
    qb = q.reshape(B, nb, Q_BLOCK, N_HEADS, 2, HEAD_DIM).transpose(1, 0, 2, 3, 4, 5)
    k_chunk = jnp.arange(S) // CHUNK

    def one_block(args):
        q_blk, i = args
        q_chunk = (i * Q_BLOCK + jnp.arange(Q_BLOCK)) // CHUNK
        mask = k_chunk[None, :] <= q_chunk[:, None]
        return _diff_attend(q_blk, k, v, lam, mask)

    out = lax.map(one_block, (qb, jnp.arange(nb)))
    return out.transpose(1, 0, 2, 3, 4).reshape(B, S, N_HEADS, V_DIM)


def _causal_depthwise_conv(u_pad, w, b):
    out = lax.conv_general_dilated(u_pad, w[:, None, :], window_strides=(1,), padding='VALID',
                                   dimension_numbers=('NWC', 'WIO', 'NWC'),
                                   feature_group_count=CONV_CH)
    return out + b


def _moe(h, w_router, b_router, w_gate, w_up, w_down):
    scores = jax.nn.softmax((h @ w_router).astype(jnp.float32), axis=-1)
    sel = scores + b_router.astype(jnp.float32)
    grp = sel.reshape(sel.shape[:-1] + (N_GROUPS, EXPERTS_PER_GROUP))
    group_score = lax.top_k(grp, TOP_K)[0].sum(-1)
    best_group = jnp.argmax(group_score, axis=-1)
    expert_group = jnp.arange(N_EXPERTS) // EXPERTS_PER_GROUP
    masked = jnp.where(expert_group == best_group[..., None], sel, -jnp.inf)
    _, idx = lax.top_k(masked, TOP_K)
    w_sel = jnp.take_along_axis(scores, idx, axis=-1)
    w_sel = w_sel / jnp.sum(w_sel, -1, keepdims=True)
    combine = jnp.sum(jax.nn.one_hot(idx, N_EXPERTS, dtype=jnp.float32) * w_sel[..., None], -2)
    hg = jnp.einsum('btd,edf->btef', h, w_gate)
    hu = jnp.einsum('btd,edf->btef', h, w_up)
    act = jax.nn.silu(hg) * hu * combine.astype(h.dtype)[..., None]
    return jnp.einsum('btef,efd->btd', act, w_down)


def _trunk_layer(x, c, pos, l, past_k, past_v, past_conv,
                 w_ada, b_ada, w_in, lambda_qk, subln_g, conv_w, conv_b, conv_ln_g, conv_ln_b,
                 w_out, ln1_g, ln1_b, ln2_g, ln2_b, w_router, b_router, w_gate, w_up, w_down):
    B, T = x.shape[0], x.shape[1]
    mod = jax.nn.silu(c) @ w_ada + b_ada
    sh_a, sc_a, g_a, sh_f, sc_f, g_f = jnp.split(mod[:, None, :], 6, axis=-1)
    h = x * (1.0 + sc_a) + sh_a
    proj = h @ w_in
    q, k, v, u = jnp.split(proj, [QK_COLS, 2 * QK_COLS, 2 * QK_COLS + ATTN_WIDTH], axis=-1)
    q = _partial_rope(q.reshape(B, T, N_HEADS, 2, HEAD_DIM), pos)
    k = _partial_rope(k.reshape(B, T, N_HEADS, 2, HEAD_DIM), pos)
    v = v.reshape(B, T, N_HEADS, V_DIM)
    lam_init = 0.8 - 0.6 * math.exp(-0.3 * l)
    lq = lambda_qk.astype(jnp.float32)
    lam = jnp.exp(jnp.sum(lq[0] * lq[1])) - jnp.exp(jnp.sum(lq[2] * lq[3])) + lam_init
    if past_k is None:
        o = _prompt_attention(q, k, v, lam)
        u_hist = jnp.zeros((B, CONV_STATE, CONV_CH), x.dtype)
    else:
        k_all = jnp.concatenate([past_k.astype(k.dtype), k], axis=1)
        v_all = jnp.concatenate([past_v.astype(v.dtype), v], axis=1)
        o = _diff_attend(q, k_all, v_all, lam, None)
        u_hist = past_conv
    o = (_rms_norm(o, subln_g) * (1.0 - lam_init)).reshape(B, T, ATTN_WIDTH)
    u_a, u_g = jnp.split(u, 2, axis=-1)
    glu = u_a * jax.nn.sigmoid(u_g)
    glu_pad = jnp.concatenate([u_hist.astype(glu.dtype), glu], axis=1)
    new_conv = glu_pad[:, -CONV_STATE:]
    cv = _causal_depthwise_conv(glu_pad, conv_w, conv_b)
    cv = jax.nn.silu(_layer_norm(cv, conv_ln_g, conv_ln_b))
    mix = jnp.concatenate([o, cv], axis=-1) @ w_out
    x = _layer_norm(DEEPNORM_ALPHA * x + g_a * mix, ln1_g, ln1_b)
    h = x * (1.0 + sc_f) + sh_f
    ffn = _moe(h, w_router, b_router, w_gate, w_up, w_down)
    x = _layer_norm(DEEPNORM_ALPHA * x + g_f * ffn, ln2_g, ln2_b)
    return x, k, v, new_conv


def setup_inputs(seed: int = 0) -> dict:
    key = jax.random.key(seed)
    ks = jax.random.split(key, 26)
    D = D_MODEL

    def nrm(k, shape, scale=1.0):
        return jax.random.normal(k, shape, jnp.float32) * scale

    return {
        'x_prompt': nrm(ks[0], (BATCH, SEQ, D)),
        'x_sample': nrm(ks[1], (DEC_BATCH, DEC_SEQ, D)),
        'cache_k': nrm(ks[2], (DEPTH, DEC_BATCH, PAST_LEN, N_HEADS, 2, HEAD_DIM)),
        'cache_v': nrm(ks[3], (DEPTH, DEC_BATCH, PAST_LEN, N_HEADS, V_DIM)),
        'state_conv': nrm(ks[4], (DEPTH, DEC_BATCH, CONV_STATE, CONV_CH), 0.5),
        'c_prompt': nrm(ks[5], (BATCH, D)),
        'c_sample': nrm(ks[6], (DEC_BATCH, D)),
        'w_ada': nrm(ks[7], (DEPTH, D, 6 * D), 0.5 * D ** -0.5),
        'b_ada': nrm(ks[8], (DEPTH, 6 * D), 0.02),
        'w_in': nrm(ks[9], (DEPTH, D, IN_COLS), D ** -0.5),
        'lambda_qk': nrm(ks[10], (DEPTH, 4, HEAD_DIM), 0.1),
        'subln_g': 1.0 + nrm(ks[11], (DEPTH, V_DIM), 0.02),
        'conv_w': nrm(ks[12], (DEPTH, CONV_WIDTH, CONV_CH), CONV_WIDTH ** -0.5),
        'conv_b': nrm(ks[13], (DEPTH, CONV_CH), 0.02),
        'conv_ln_g': 1.0 + nrm(ks[14], (DEPTH, CONV_CH), 0.02),
        'conv_ln_b': nrm(ks[15], (DEPTH, CONV_CH), 0.02),
        'w_out': nrm(ks[16], (DEPTH, MIX_WIDTH, D), DEEPNORM_BETA * MIX_WIDTH ** -0.5),
        'ln1_g': 1.0 + nrm(ks[17], (DEPTH, D), 0.02),
        'ln1_b': nrm(ks[18], (DEPTH, D), 0.02),
        'ln2_g': 1.0 + nrm(ks[19], (DEPTH, D), 0.02),
        'ln2_b': nrm(ks[20], (DEPTH, D), 0.02),
        'w_router': nrm(ks[21], (D, N_EXPERTS), D ** -0.5),
        'b_router': nrm(ks[22], (N_EXPERTS,), 0.01),
        'w_gate': nrm(ks[23], (DEPTH, N_EXPERTS, D, D_EXPERT), D ** -0.5),
        'w_up': nrm(ks[24], (DEPTH, N_EXPERTS, D, D_EXPERT), D ** -0.5),
        'w_down': nrm(ks[25], (DEPTH, N_EXPERTS, D_EXPERT, D), DEEPNORM_BETA * D_EXPERT ** -0.5),
    }


def reference(x_prompt, x_sample, cache_k, cache_v, state_conv, c_prompt, c_sample,
              w_ada, b_ada, w_in, lambda_qk, subln_g, conv_w, conv_b, conv_ln_g, conv_ln_b,
              w_out, ln1_g, ln1_b, ln2_g, ln2_b, w_router, b_router, w_gate, w_up, w_down):
    pos_p = jnp.arange(x_prompt.shape[1], dtype=jnp.int32)
    pos_s = PAST_LEN + jnp.arange(x_sample.shape[1], dtype=jnp.int32)
    xp, xs = x_prompt, x_sample
    k_p, v_p, cv_p, k_s, v_s, cv_s = [], [], [], [], [], []
    for l in range(DEPTH):
        lw = (w_ada[l], b_ada[l], w_in[l], lambda_qk[l], subln_g[l], conv_w[l], conv_b[l],
              conv_ln_g[l], conv_ln_b[l], w_out[l], ln1_g[l], ln1_b[l], ln2_g[l], ln2_b[l],
              w_router, b_router, w_gate[l], w_up[l], w_down[l])
        xp, kn, vn, cn = _trunk_layer(xp, c_prompt, pos_p, l, None, None, None, *lw)
        k_p.append(kn)
        v_p.append(vn)
        cv_p.append(cn)
        xs, kn, vn, cn = _trunk_layer(xs, c_sample, pos_s, l, cache_k[l], cache_v[l], state_conv[l], *lw)
        k_s.append(kn)
        v_s.append(vn)
        cv_s.append(cn)
    return (xp, xs, jnp.stack(k_p), jnp.stack(v_p), jnp.stack(cv_p),
            jnp.stack(k_s), jnp.stack(v_s), jnp.stack(cv_s))
```

```python
import functools
import math

import jax
import jax.numpy as jnp
from jax import lax
from jax.experimental import pallas as pl
from jax.experimental.pallas import tpu as pltpu

F32 = jnp.float32
BF16 = jnp.bfloat16

N_HEADS = 4
HEAD_DIM = 64
V_DIM = 2 * HEAD_DIM
ATTN_WIDTH = N_HEADS * V_DIM
QK_COLS = N_HEADS * 2 * HEAD_DIM
ROT_DIM = HEAD_DIM // 4
ROT_HALF = ROT_DIM // 2
ROPE_THETA = 500000.0
CHUNK = 64
CONV_WIDTH = 31
CONV_STATE = CONV_WIDTH - 1
N_EXPERTS = 16
N_GROUPS = 4
EXPERTS_PER_GROUP = N_EXPERTS // N_GROUPS
LN_EPS = 1e-5
LANES = 128
HALO = 32
NEG_BIG = -1e30
VMEM_LIMIT = 52 * 1024 * 1024


def _cparams(sem):
    return pltpu.CompilerParams(dimension_semantics=sem, vmem_limit_bytes=VMEM_LIMIT)


def _silu(x):
    return x * jax.nn.sigmoid(x)


def _layer_norm(y, g, b):
    mu = jnp.mean(y, axis=-1, keepdims=True)
    yc = y - mu
    var = jnp.mean(yc * yc, axis=-1, keepdims=True)
    return yc * lax.rsqrt(var + LN_EPS) * g + b


def _ada_kernel(c_ref, w_ref, b_ref, o_ref):
    s = _silu(c_ref[...])
    o_ref[0] = jnp.dot(s, w_ref[0], preferred_element_type=F32,
                       precision=lax.Precision.HIGHEST) + b_ref[0]


def _ada_all(c_all, w_ada, b_ada):
    depth, d, n6 = w_ada.shape
    rows = c_all.shape[0]
    tn = d
    return pl.pallas_call(
        _ada_kernel,
        out_shape=jax.ShapeDtypeStruct((depth, rows, n6), F32),
        grid=(depth, n6 // tn),
        in_specs=[
            pl.BlockSpec((rows, d), lambda l, j: (0, 0)),
            pl.BlockSpec((1, d, tn), lambda l, j: (l, 0, j)),
            pl.BlockSpec((1, 1, tn), lambda l, j: (l, 0, j)),
        ],
        out_specs=pl.BlockSpec((1, rows, tn), lambda l, j: (l, 0, j)),
        compiler_params=_cparams(("arbitrary", "arbitrary")),
        name="ada_ln",
    )(c_all, w_ada, b_ada.reshape(depth, 1, n6))


def _rope_tables(pos):
    inv_freq = jnp.power(ROPE_THETA, -jnp.arange(ROT_HALF, dtype=F32) * 2.0 / ROT_DIM)
    ang = pos.astype(F32)[:, None] * inv_freq[None, :]
    cos, sin = jnp.cos(ang), jnp.sin(ang)
    t = pos.shape[0]
    one = jnp.ones((t, HEAD_DIM - ROT_DIM), F32)
    zero8 = jnp.zeros((t, ROT_HALF), F32)
    zero = jnp.zeros((t, HEAD_DIM - ROT_DIM), F32)
    cm = jnp.concatenate([cos, cos, one], axis=1)
    s1 = jnp.concatenate([-sin, zero8, zero], axis=1)
    s2 = jnp.concatenate([zero8, sin, zero], axis=1)
    rep = LANES // HEAD_DIM
    return tuple(jnp.tile(a, (1, rep)) for a in (cm, s1, s2))


def _in_kernel(x_ref, mod_ref, w_ref, cm_ref, s1_ref, s2_ref,
               q_ref, kb_ref, vb_ref, k32_ref, v32_ref, glu_ref):
    x = x_ref[...]
    h = (x * (1.0 + mod_ref[1:2, :]) + mod_ref[0:1, :]).astype(BF16)
    cm, s1, s2 = cm_ref[...], s1_ref[...], s2_ref[...]

    def rope(p):
        outs = []
        for c in range(p.shape[1] // LANES):
            pc = p[:, c * LANES:(c + 1) * LANES]
            fwd = pltpu.roll(pc, LANES - ROT_HALF, axis=1)
            bwd = pltpu.roll(pc, ROT_HALF, axis=1)
            outs.append(pc * cm + fwd * s1 + bwd * s2)
        return jnp.concatenate(outs, axis=1)

    def proj(c0, n):
        return jnp.dot(h, w_ref[:, c0:c0 + n], preferred_element_type=F32)

    q = rope(proj(0, QK_COLS)) * (HEAD_DIM ** -0.5)
    q_ref[...] = q.astype(BF16)
    k = rope(proj(QK_COLS, QK_COLS))
    k32_ref[...] = k
    kb_ref[...] = k.astype(BF16)
    v = proj(2 * QK_COLS, ATTN_WIDTH)
    v32_ref[...] = v
    vb_ref[...] = v.astype(BF16)
    conv_ch = glu_ref.shape[1]
    u0 = 2 * QK_COLS + ATTN_WIDTH
    ua = proj(u0, conv_ch)
    ug = proj(u0 + conv_ch, conv_ch)
    glu_ref[...] = ua * jax.nn.sigmoid(ug)


def _in_proj(x, mod, w_in_bf, tables, layer, seq, tm):
    t, d = x.shape
    conv_ch = d - ATTN_WIDTH
    in_cols = w_in_bf.shape[2]
    tiles_per_seq = seq // tm
    row = lambda i: (i, 0)
    tab = pl.BlockSpec((tm, LANES), lambda i: (i % tiles_per_seq, 0))
    outs = pl.pallas_call(
        _in_kernel,
        out_shape=(
            jax.ShapeDtypeStruct((t, QK_COLS), BF16),
            jax.ShapeDtypeStruct((t, QK_COLS), BF16),
            jax.ShapeDtypeStruct((t, ATTN_WIDTH), BF16),
            jax.ShapeDtypeStruct((t, QK_COLS), F32),
            jax.ShapeDtypeStruct((t, ATTN_WIDTH), F32),
            jax.ShapeDtypeStruct((t, conv_ch), F32),
        ),
        grid=(t // tm,),
        in_specs=[
            pl.BlockSpec((tm, d), row),
            pl.BlockSpec((None, 6, d), lambda i: (i // tiles_per_seq, 0, 0)),
            pl.BlockSpec((None, d, in_cols), lambda i: (layer, 0, 0)),
            tab, tab, tab,
        ],
        out_specs=(
            pl.BlockSpec((tm, QK_COLS), row),
            pl.BlockSpec((tm, QK_COLS), row),
            pl.BlockSpec((tm, ATTN_WIDTH), row),
            pl.BlockSpec((tm, QK_COLS), row),
            pl.BlockSpec((tm, ATTN_WIDTH), row),
            pl.BlockSpec((tm, conv_ch), row),
        ),
        compiler_params=_cparams(("arbitrary",)),
        name="in_proj",
    )(x, mod, w_in_bf, *tables)
    return outs


def _stack_maps(q):
    lane = lax.broadcasted_iota(jnp.int32, q.shape, 1)
    zero = jnp.zeros_like(q)
    return jnp.concatenate([jnp.where(lane < HEAD_DIM, q, zero),
                            jnp.where(lane >= HEAD_DIM, q, zero)], axis=0)


def _lambda(lam_ref, lam_init):
    lq = lam_ref[...]
    t1 = jnp.sum(lq[0:1, :] * lq[1:2, :], axis=-1, keepdims=True)
    t2 = jnp.sum(lq[2:3, :] * lq[3:4, :], axis=-1, keepdims=True)
    return jnp.exp(t1) - jnp.exp(t2) + lam_init


def _finish_heads(o1, o2, lam, g_ref, lam_init):
    o = o1 - lam * o2
    ms = jnp.mean(o * o, axis=-1, keepdims=True)
    return o * lax.rsqrt(ms + LN_EPS) * g_ref[...] * (1.0 - lam_init)


def _scores(qs, kblk):
    return lax.dot_general(qs, kblk, (((1,), (1,)), ((), ())), preferred_element_type=F32)


def _attn_prompt_kernel(q_ref, k_ref, v_ref, lam_ref, g_ref, o_ref, m_ref, l_ref, acc_ref,
                        *, lam_init, tq):
    qi = pl.program_id(2)
    qs = _stack_maps(q_ref[...])
    m_ref[...] = jnp.full(m_ref.shape, NEG_BIG, F32)
    l_ref[...] = jnp.zeros(l_ref.shape, F32)
    acc_ref[...] = jnp.zeros(acc_ref.shape, F32)

    def block(start, masked):
        kblk = k_ref[pl.ds(start, tq), :]
        vblk = v_ref[pl.ds(start, tq), :]
        s = _scores(qs, kblk)
        if masked:
            r = lax.broadcasted_iota(jnp.int32, s.shape, 0)
            c = lax.broadcasted_iota(jnp.int32, s.shape, 1)
            shift = CHUNK.bit_length() - 1
            qc = lax.shift_right_logical(jnp.where(r >= tq, r - tq, r), shift)
            s = jnp.where(lax.shift_right_logical(c, shift) <= qc, s, NEG_BIG)
        m_prev = m_ref[...]
        m_new = jnp.maximum(m_prev, jnp.max(s, axis=-1, keepdims=True))
        alpha = jnp.exp(m_prev - m_new)
        p = jnp.exp(s - m_new)
        l_ref[...] = alpha * l_ref[...] + jnp.sum(p, axis=-1, keepdims=True)
        acc_ref[...] = alpha * acc_ref[...] + jnp.dot(p.astype(BF16), vblk,
                                                      preferred_element_type=F32)
        m_ref[...] = m_new

    def body(j, carry):
        block(pl.multiple_of(j * tq, tq), False)
        return carry

    lax.fori_loop(0, qi, body, 0)
    block(pl.multiple_of(qi * tq, tq), True)

    o = acc_ref[...] / l_ref[...]
    lam = _lambda(lam_ref, lam_init)
    o_ref[...] = _finish_heads(o[:tq], o[tq:], lam, g_ref, lam_init).astype(o_ref.dtype)


def _attn_prompt(q, k, v, lambda_qk, subln_g, layer, lam_init, tq):
    b, s, _ = q.shape
    kv = pl.BlockSpec((None, s, V_DIM), lambda bi, h, qi: (bi, 0, h))
    return pl.pallas_call(
        functools.partial(_attn_prompt_kernel, lam_init=lam_init, tq=tq),
        out_shape=jax.ShapeDtypeStruct((b, s, ATTN_WIDTH), BF16),
        grid=(b, N_HEADS, s // tq),
        in_specs=[
            pl.BlockSpec((None, tq, V_DIM), lambda bi, h, qi: (bi, qi, h)),
            kv, kv,
            pl.BlockSpec((None, 4, HEAD_DIM), lambda bi, h, qi: (layer, 0, 0)),
            pl.BlockSpec((None, 1, V_DIM), lambda bi, h, qi: (layer, 0, 0)),
        ],
        out_specs=pl.BlockSpec((None, tq, V_DIM), lambda bi, h, qi: (bi, qi, h)),
        scratch_shapes=[
            pltpu.VMEM((2 * tq, 1), F32),
            pltpu.VMEM((2 * tq, 1), F32),
            pltpu.VMEM((2 * tq, V_DIM), F32),
        ],
        compiler_params=_cparams(("arbitrary", "arbitrary", "arbitrary")),
        name="attn_prompt",
    )(q, k, v, lambda_qk, subln_g.reshape(subln_g.shape[0], 1, V_DIM))


def _attn_sample_kernel(q_ref, kn_ref, vn_ref, ck_ref, cv_ref, lam_ref, g_ref, o_ref, *, lam_init):
    tq = q_ref.shape[0]
    qs = _stack_maps(q_ref[...])
    s_p = _scores(qs, ck_ref[...].astype(BF16))
    s_n = _scores(qs, kn_ref[...])
    m = jnp.maximum(jnp.max(s_p, axis=-1, keepdims=True), jnp.max(s_n, axis=-1, keepdims=True))
    p_p = jnp.exp(s_p - m)
    p_n = jnp.exp(s_n - m)
    l = jnp.sum(p_p, axis=-1, keepdims=True) + jnp.sum(p_n, axis=-1, keepdims=True)
    acc = jnp.dot(p_p.astype(BF16), cv_ref[...].astype(BF16), preferred_element_type=F32)
    acc = acc + jnp.dot(p_n.astype(BF16), vn_ref[...], preferred_element_type=F32)
    o = acc / l
    lam = _lambda(lam_ref, lam_init)
    o_ref[...] = _finish_heads(o[:tq], o[tq:], lam, g_ref, lam_init).astype(o_ref.dtype)


def _attn_sample(q, kn, vn, cache_k, cache_v, lambda_qk, subln_g, layer, lam_init):
    b, t, _ = q.shape
    past = cache_k.shape[2]
    new = pl.BlockSpec((None, t, V_DIM), lambda bi, h: (bi, 0, h))
    old = pl.BlockSpec((None, None, past, V_DIM), lambda bi, h: (layer, bi, 0, h))
    return pl.pallas_call(
        functools.partial(_attn_sample_kernel, lam_init=lam_init),
        out_shape=jax.ShapeDtypeStruct((b, t, ATTN_WIDTH), BF16),
        grid=(b, N_HEADS),
        in_specs=[
            new, new, new, old, old,
            pl.BlockSpec((None, 4, HEAD_DIM), lambda bi, h: (layer, 0, 0)),
            pl.BlockSpec((None, 1, V_DIM), lambda bi, h: (layer, 0, 0)),
        ],
        out_specs=new,
        compiler_params=_cparams(("arbitrary", "arbitrary")),
        name="attn_sample",
    )(q, kn, vn, cache_k, cache_v, lambda_qk, subln_g.reshape(subln_g.shape[0], 1, V_DIM))


def _route(h2, wr_ref, br_ref):
    logits = jnp.dot(h2, wr_ref[...], preferred_element_type=F32, precision=lax.Precision.HIGHEST)
    mx = jnp.max(logits, axis=-1, keepdims=True)
    ex = jnp.exp(logits - mx)
    scores = ex / jnp.sum(ex, axis=-1, keepdims=True)
    sel = scores + br_ref[...]
    lane_i = lax.broadcasted_iota(jnp.int32, sel.shape, 1)
    group = lax.shift_right_logical(lane_i, EXPERTS_PER_GROUP.bit_length() - 1).astype(F32)
    lane = lane_i.astype(F32)
    neg = jnp.full_like(sel, -jnp.inf)
    none = float(N_EXPERTS)

    def top2(mask):
        m1 = jnp.max(jnp.where(mask, sel, neg), axis=-1, keepdims=True)
        i1 = jnp.min(jnp.where(mask & (sel == m1), lane, none), axis=-1, keepdims=True)
        rest = mask & (lane != i1)
        m2 = jnp.max(jnp.where(rest, sel, neg), axis=-1, keepdims=True)
        i2 = jnp.min(jnp.where(rest & (sel == m2), lane, none), axis=-1, keepdims=True)
        return m1, i1, m2, i2

    best_score = None
    best_group = None
    for g in range(N_GROUPS):
        m1, _, m2, _ = top2(group == float(g))
        gs = m1 + m2
        if g == 0:
            best_score, best_group = gs, jnp.zeros_like(gs)
        else:
            better = gs > best_score
            best_score = jnp.where(better, gs, best_score)
            best_group = jnp.where(better, float(g), best_group)
    _, i1, _, i2 = top2(group == best_group)
    chosen = (lane == i1) | (lane == i2)
    w = jnp.where(chosen, scores, 0.0)
    return w / jnp.sum(w, axis=-1, keepdims=True)


def _tail_kernel(glu_ref, halo_ref, o_ref, x_ref, mod_ref, cw_ref, cb_ref, cg_ref, cbeta_ref,
                 wo_ref, g1_ref, b1_ref, wr_ref, br_ref,
                 x1_ref, h2_ref, comb_ref, pad_ref, mixin_ref, *, alpha, zero_first_halo,
                 tiles_per_seq):
    tm = glu_ref.shape[0]
    halo = halo_ref[...]
    if zero_first_halo:
        first = (pl.program_id(0) % tiles_per_seq) == 0
        halo = jnp.where(first, jnp.zeros_like(halo), halo)
    pad_ref[0:HALO, :] = halo
    pad_ref[HALO:, :] = glu_ref[...]
    off = HALO - CONV_STATE
    rows = 64
    for r0 in range(0, tm, rows):
        acc = pad_ref[r0 + off:r0 + off + rows, :] * cw_ref[0:1, :]
        for j in range(1, CONV_WIDTH):
            acc = acc + pad_ref[r0 + off + j:r0 + off + j + rows, :] * cw_ref[j:j + 1, :]
        cv = _silu(_layer_norm(acc + cb_ref[...], cg_ref[...], cbeta_ref[...]))
        mixin_ref[r0:r0 + rows, ATTN_WIDTH:] = cv.astype(BF16)
    mixin_ref[:, :ATTN_WIDTH] = o_ref[...]
    mix = jnp.dot(mixin_ref[...], wo_ref[...], preferred_element_type=F32)
    y = alpha * x_ref[...] + mod_ref[2:3, :] * mix
    x1 = _layer_norm(y, g1_ref[...], b1_ref[...])
    x1_ref[...] = x1
    h2 = x1 * (1.0 + mod_ref[4:5, :]) + mod_ref[3:4, :]
    h2_ref[...] = h2.astype(BF16)
    comb_ref[...] = _route(h2, wr_ref, br_ref)


def _mixer_tail(glu, halo_src, o, x, mod, conv_w, conv_b, conv_ln_g, conv_ln_b, w_out_bf,
                ln1_g, ln1_b, w_router, b_router, layer, seq, tm, alpha, halo_is_glu):
    t, d = x.shape
    conv_ch = glu.shape[1]
    tiles_per_seq = seq // tm
    depth = conv_w.shape[0]
    row = lambda i: (i, 0)
    vec = lambda n: pl.BlockSpec((None, 1, n), lambda i: (layer, 0, 0))
    if halo_is_glu:
        per = tm // HALO
        halo_spec = pl.BlockSpec((HALO, conv_ch), lambda i: (jnp.maximum(i * per - 1, 0), 0))
    else:
        halo_spec = pl.BlockSpec((None, HALO, conv_ch), lambda i: (i, 0, 0))
    return pl.pallas_call(
        functools.partial(_tail_kernel, alpha=alpha, zero_first_halo=halo_is_glu,
                          tiles_per_seq=tiles_per_seq),
        out_shape=(
            jax.ShapeDtypeStruct((t, d), F32),
            jax.ShapeDtypeStruct((t, d), BF16),
            jax.ShapeDtypeStruct((t, N_EXPERTS), F32),
        ),
        grid=(t // tm,),
        in_specs=[
            pl.BlockSpec((tm, conv_ch), row),
            halo_spec,
            pl.BlockSpec((tm, ATTN_WIDTH), row),
            pl.BlockSpec((tm, d), row),
            pl.BlockSpec((None, 6, d), lambda i: (i // tiles_per_seq, 0, 0)),
            pl.BlockSpec((None, CONV_WIDTH, conv_ch), lambda i: (layer, 0, 0)),
            vec(conv_ch), vec(conv_ch), vec(conv_ch),
            pl.BlockSpec((None, d, d), lambda i: (layer, 0, 0)),
            vec(d), vec(d),
            pl.BlockSpec((d, N_EXPERTS), lambda i: (0, 0)),
            pl.BlockSpec((1, N_EXPERTS), lambda i: (0, 0)),
        ],
        out_specs=(
            pl.BlockSpec((tm, d), row),
            pl.BlockSpec((tm, d), row),
            pl.BlockSpec((tm, N_EXPERTS), row),
        ),
        scratch_shapes=[pltpu.VMEM((tm + HALO, conv_ch), F32), pltpu.VMEM((tm, d), BF16)],
        compiler_params=_cparams(("arbitrary",)),
        name="mixer_tail",
    )(glu, halo_src, o, x, mod, conv_w,
      conv_b.reshape(depth, 1, conv_ch), conv_ln_g.reshape(depth, 1, conv_ch),
      conv_ln_b.reshape(depth, 1, conv_ch), w_out_bf,
      ln1_g.reshape(depth, 1, d), ln1_b.reshape(depth, 1, d),
      w_router, b_router.reshape(1, N_EXPERTS))


def _moe_kernel(h_ref, comb_ref, x_ref, mod_ref, wg_ref, wu_ref, wd_ref, g2_ref, b2_ref,
                o_ref, acc_ref, *, alpha, nseg):
    e = pl.program_id(1)

    @pl.when(e == 0)
    def _():
        acc_ref[...] = jnp.zeros(acc_ref.shape, F32)

    h = h_ref[...]
    comb = comb_ref[...]
    lane = lax.broadcasted_iota(jnp.int32, comb.shape, 1)
    c = jnp.sum(jnp.where(lane == e, comb, 0.0), axis=-1, keepdims=True)
    hg = jnp.dot(h, wg_ref[...], preferred_element_type=F32)
    hu = jnp.dot(h, wu_ref[...], preferred_element_type=F32)
    act = (_silu(hg) * hu * c).astype(BF16)
    acc_ref[...] += jnp.dot(act, wd_ref[...], preferred_element_type=F32)

    @pl.when(e == N_EXPERTS - 1)
    def _():
        tm, d = acc_ref.shape
        seg = tm // nseg
        gate = jnp.concatenate(
            [jnp.broadcast_to(mod_ref[s, 5:6, :], (seg, d)) for s in range(nseg)], axis=0)
        y = alpha * x_ref[...] + gate * acc_ref[...]
        o_ref[...] = _layer_norm(y, g2_ref[...], b2_ref[...])


def _moe(h2, comb, x1, mod, wg_bf, wu_bf, wd_bf, ln2_g, ln2_b, layer, seq, tm, alpha):
    t, d = x1.shape
    dexp = wg_bf.shape[3]
    depth = wg_bf.shape[0]
    nseg = max(tm // seq, 1)
    tiles_per_seq = max(seq // tm, 1)
    row = lambda i, e: (i, 0)
    vec = pl.BlockSpec((None, 1, d), lambda i, e: (layer, 0, 0))
    return pl.pallas_call(
        functools.partial(_moe_kernel, alpha=alpha, nseg=nseg),
        out_shape=jax.ShapeDtypeStruct((t, d), F32),
        grid=(t // tm, N_EXPERTS),
        in_specs=[
            pl.BlockSpec((tm, d), row),
            pl.BlockSpec((tm, N_EXPERTS), row),
            pl.BlockSpec((tm, d), row),
            pl.BlockSpec((nseg, 6, d), lambda i, e: (i // tiles_per_seq, 0, 0)),
            pl.BlockSpec((None, None, d, dexp), lambda i, e: (layer, e, 0, 0)),
            pl.BlockSpec((None, None, d, dexp), lambda i, e: (layer, e, 0, 0)),
            pl.BlockSpec((None, None, dexp, d), lambda i, e: (layer, e, 0, 0)),
            vec, vec,
        ],
        out_specs=pl.BlockSpec((tm, d), row),
        scratch_shapes=[pltpu.VMEM((tm, d), F32)],
        compiler_params=_cparams(("arbitrary", "arbitrary")),
        name="moe",
    )(h2, comb, x1, mod, wg_bf, wu_bf, wd_bf,
      ln2_g.reshape(depth, 1, d), ln2_b.reshape(depth, 1, d))


def _pick_tile(n, want):
    t = min(n, want)
    while n % t:
        t //= 2
    return t


def kernel(x_prompt, x_sample, cache_k, cache_v, state_conv, c_prompt, c_sample, w_ada, b_ada, w_in, lambda_qk, subln_g, conv_w, conv_b, conv_ln_g, conv_ln_b, w_out, ln1_g, ln1_b, ln2_g, ln2_b, w_router, b_router, w_gate, w_up, w_down):
    bp, sp, d = x_prompt.shape
    bs, ss, _ = x_sample.shape
    depth = w_ada.shape[0]
    past = cache_k.shape[2]
    conv_ch = d - ATTN_WIDTH
    alpha = (2 * depth) ** 0.25

    mod = _ada_all(jnp.concatenate([c_prompt, c_sample], axis=0), w_ada, b_ada)
    mod = mod.reshape(depth, bp + bs, 6, d)

    w_in_bf = w_in.astype(BF16)
    w_out_bf = w_out.astype(BF16)
    wg_bf, wu_bf, wd_bf = w_gate.astype(BF16), w_up.astype(BF16), w_down.astype(BF16)

    tab_p = _rope_tables(jnp.arange(sp, dtype=jnp.int32))
    tab_s = _rope_tables(past + jnp.arange(ss, dtype=jnp.int32))
    ck = cache_k.reshape(depth, bs, past, QK_COLS)
    cv = cache_v.reshape(depth, bs, past, ATTN_WIDTH)
    hist = jnp.concatenate(
        [jnp.zeros((depth, bs, HALO - CONV_STATE, conv_ch), F32), state_conv], axis=2)

    tm_p = _pick_tile(sp, 512)
    tq = _pick_tile(sp, 256)
    tm_moe_p = _pick_tile(sp, 1024)

    xp = x_prompt.reshape(bp * sp, d)
    xs = x_sample.reshape(bs * ss, d)
    outs = {n: [] for n in ("kp", "vp", "cp", "ks", "vs", "cs")}
    for l in range(depth):
        lam_init = 0.8 - 0.6 * math.exp(-0.3 * l)
        mod_p, mod_s = mod[l, :bp], mod[l, bp:]

        q, kb, vb, k32, v32, glu = _in_proj(xp, mod_p, w_in_bf, tab_p, l, sp, tm_p)
        o = _attn_prompt(q.reshape(bp, sp, QK_COLS), kb.reshape(bp, sp, QK_COLS),
                         vb.reshape(bp, sp, ATTN_WIDTH), lambda_qk, subln_g, l, lam_init, tq)
        x1, h2, comb = _mixer_tail(glu, glu, o.reshape(bp * sp, ATTN_WIDTH), xp, mod_p,
                                   conv_w, conv_b, conv_ln_g, conv_ln_b, w_out_bf, ln1_g, ln1_b,
                                   w_router, b_router, l, sp, tm_p, alpha, True)
        xp = _moe(h2, comb, x1, mod_p, wg_bf, wu_bf, wd_bf, ln2_g, ln2_b, l, sp, tm_moe_p, alpha)
        outs["kp"].append(k32.reshape(bp, sp, N_HEADS, 2, HEAD_DIM))
        outs["vp"].append(v32.reshape(bp, sp, N_HEADS, V_DIM))
        outs["cp"].append(glu.reshape(bp, sp, conv_ch)[:, sp - CONV_STATE:])

        q, kb, vb, k32, v32, glu = _in_proj(xs, mod_s, w_in_bf, tab_s, l, ss, ss)
        o = _attn_sample(q.reshape(bs, ss, QK_COLS), kb.reshape(bs, ss, QK_COLS),
                         vb.reshape(bs, ss, ATTN_WIDTH), ck, cv, lambda_qk, subln_g, l, lam_init)
        x1, h2, comb = _mixer_tail(glu, hist[l], o.reshape(bs * ss, ATTN_WIDTH), xs, mod_s,
                                   conv_w, conv_b, conv_ln_g, conv_ln_b, w_out_bf, ln1_g, ln1_b,
                                   w_router, b_router, l, ss, ss, alpha, False)
        xs = _moe(h2, comb, x1, mod_s, wg_bf, wu_bf, wd_bf, ln2_g, ln2_b, l, ss, bs * ss, alpha)
        outs["ks"].append(k32.reshape(bs, ss, N_HEADS, 2, HEAD_DIM))
        outs["vs"].append(v32.reshape(bs, ss, N_HEADS, V_DIM))
        new_conv = jnp.concatenate([state_conv[l], glu.reshape(bs, ss, conv_ch)], axis=1)
        outs["cs"].append(new_conv[:, -CONV_STATE:])

    return (xp.reshape(bp, sp, d), xs.reshape(bs, ss, d),
            jnp.stack(outs["kp"]), jnp.stack(outs["vp"]), jnp.stack(outs["cp"]),
            jnp.stack(outs["ks"]), jnp.stack(outs["vs"]), jnp.stack(outs["cs"]))
```

```python
import functools
import math

import jax
import jax.numpy as jnp
from jax import lax
from jax.experimental import pallas as pl
from jax.experimental.pallas import tpu as pltpu

F32 = jnp.float32
BF16 = jnp.bfloat16

N_HEADS = 4
HEAD_DIM = 64
V_DIM = 2 * HEAD_DIM
ATTN_WIDTH = N_HEADS * V_DIM
QK_COLS = N_HEADS * 2 * HEAD_DIM
ROT_DIM = HEAD_DIM // 4
ROT_HALF = ROT_DIM // 2
ROPE_THETA = 500000.0
CHUNK = 64
CONV_WIDTH = 31
CONV_STATE = CONV_WIDTH - 1
N_EXPERTS = 16
N_GROUPS = 4
EXPERTS_PER_GROUP = N_EXPERTS // N_GROUPS
LN_EPS = 1e-5
LANES = 128
HALO = 32
NEG_BIG = -1e30
VMEM_LIMIT = 52 * 1024 * 1024


def _cparams(sem):
    return pltpu.CompilerParams(dimension_semantics=sem, vmem_limit_bytes=VMEM_LIMIT)


def _silu(x):
    return x * jax.nn.sigmoid(x)


def _layer_norm(y, g, b):
    mu = jnp.mean(y, axis=-1, keepdims=True)
    yc = y - mu
    var = jnp.mean(yc * yc, axis=-1, keepdims=True)
    return yc * lax.rsqrt(var + LN_EPS) * g + b


def _ada_kernel(c_ref, w_ref, b_ref, o_ref):
    s = _silu(c_ref[...])
    o_ref[0] = jnp.dot(s, w_ref[0], preferred_element_type=F32,
                       precision=lax.Precision.HIGHEST) + b_ref[0]


def _ada_all(c_all, w_ada, b_ada):
    depth, d, n6 = w_ada.shape
    rows = c_all.shape[0]
    tn = d
    return pl.pallas_call(
        _ada_kernel,
        out_shape=jax.ShapeDtypeStruct((depth, rows, n6), F32),
        grid=(depth, n6 // tn),
        in_specs=[
            pl.BlockSpec((rows, d), lambda l, j: (0, 0)),
            pl.BlockSpec((1, d, tn), lambda l, j: (l, 0, j)),
            pl.BlockSpec((1, 1, tn), lambda l, j: (l, 0, j)),
        ],
        out_specs=pl.BlockSpec((1, rows, tn), lambda l, j: (l, 0, j)),
        compiler_params=_cparams(("arbitrary", "arbitrary")),
        name="ada_ln",
    )(c_all, w_ada, b_ada.reshape(depth, 1, n6))


def _rope_tables(pos):
    inv_freq = jnp.power(ROPE_THETA, -jnp.arange(ROT_HALF, dtype=F32) * 2.0 / ROT_DIM)
    ang = pos.astype(F32)[:, None] * inv_freq[None, :]
    cos, sin = jnp.cos(ang), jnp.sin(ang)
    t = pos.shape[0]
    one = jnp.ones((t, HEAD_DIM - ROT_DIM), F32)
    zero8 = jnp.zeros((t, ROT_HALF), F32)
    zero = jnp.zeros((t, HEAD_DIM - ROT_DIM), F32)
    cm = jnp.concatenate([cos, cos, one], axis=1)
    s1 = jnp.concatenate([-sin, zero8, zero], axis=1)
    s2 = jnp.concatenate([zero8, sin, zero], axis=1)
    rep = LANES // HEAD_DIM
    return tuple(jnp.tile(a, (1, rep)) for a in (cm, s1, s2))


def _in_kernel(x_ref, mod_ref, w_ref, cm_ref, s1_ref, s2_ref,
               q_ref, kb_ref, vb_ref, k32_ref, v32_ref, glu_ref, *, v_transposed):
    x = x_ref[...]
    h = (x * (1.0 + mod_ref[1:2, :]) + mod_ref[0:1, :]).astype(BF16)
    cm, s1, s2 = cm_ref[...], s1_ref[...], s2_ref[...]

    def rope(p):
        outs = []
        for c in range(p.shape[1] // LANES):
            pc = p[:, c * LANES:(c + 1) * LANES]
            fwd = pltpu.roll(pc, LANES - ROT_HALF, axis=1)
            bwd = pltpu.roll(pc, ROT_HALF, axis=1)
            outs.append(pc * cm + fwd * s1 + bwd * s2)
        return jnp.concatenate(outs, axis=1)

    def proj(c0, n):
        return jnp.dot(h, w_ref[:, c0:c0 + n], preferred_element_type=F32)

    q = rope(proj(0, QK_COLS)) * (HEAD_DIM ** -0.5)
    q_ref[...] = q.astype(BF16)
    k = rope(proj(QK_COLS, QK_COLS))
    k32_ref[...] = k
    kb_ref[...] = k.astype(BF16)
    v = proj(2 * QK_COLS, ATTN_WIDTH)
    v32_ref[...] = v
    if v_transposed:
        vb_ref[0] = v.T.astype(BF16)
    else:
        vb_ref[...] = v.astype(BF16)
    conv_ch = glu_ref.shape[1]
    u0 = 2 * QK_COLS + ATTN_WIDTH
    ua = proj(u0, conv_ch)
    ug = proj(u0 + conv_ch, conv_ch)
    glu_ref[...] = ua * jax.nn.sigmoid(ug)


def _in_proj(x, mod, w_in_bf, tables, layer, seq, tm, v_transposed):
    t, d = x.shape
    conv_ch = d - ATTN_WIDTH
    in_cols = w_in_bf.shape[2]
    tiles_per_seq = seq // tm
    row = lambda i: (i, 0)
    tab = pl.BlockSpec((tm, LANES), lambda i: (i % tiles_per_seq, 0))
    if v_transposed:
        vb_shape = jax.ShapeDtypeStruct((t // tm, ATTN_WIDTH, tm), BF16)
        vb_spec = pl.BlockSpec((1, ATTN_WIDTH, tm), lambda i: (i, 0, 0))
    else:
        vb_shape = jax.ShapeDtypeStruct((t, ATTN_WIDTH), BF16)
        vb_spec = pl.BlockSpec((tm, ATTN_WIDTH), row)
    outs = pl.pallas_call(
        functools.partial(_in_kernel, v_transposed=v_transposed),
        out_shape=(
            jax.ShapeDtypeStruct((t, QK_COLS), BF16),
            jax.ShapeDtypeStruct((t, QK_COLS), BF16),
            vb_shape,
            jax.ShapeDtypeStruct((t, QK_COLS), F32),
            jax.ShapeDtypeStruct((t, ATTN_WIDTH), F32),
            jax.ShapeDtypeStruct((t, conv_ch), F32),
        ),
        grid=(t // tm,),
        in_specs=[
            pl.BlockSpec((tm, d), row),
            pl.BlockSpec((None, 6, d), lambda i: (i // tiles_per_seq, 0, 0)),
            pl.BlockSpec((None, d, in_cols), lambda i: (layer, 0, 0)),
            tab, tab, tab,
        ],
        out_specs=(
            pl.BlockSpec((tm, QK_COLS), row),
            pl.BlockSpec((tm, QK_COLS), row),
            vb_spec,
            pl.BlockSpec((tm, QK_COLS), row),
            pl.BlockSpec((tm, ATTN_WIDTH), row),
            pl.BlockSpec((tm, conv_ch), row),
        ),
        compiler_params=_cparams(("arbitrary",)),
        name="in_proj",
    )(x, mod, w_in_bf, *tables)
    return outs


def _stack_maps(q):
    lane = lax.broadcasted_iota(jnp.int32, q.shape, 1)
    zero = jnp.zeros_like(q)
    return jnp.concatenate([jnp.where(lane < HEAD_DIM, q, zero),
                            jnp.where(lane >= HEAD_DIM, q, zero)], axis=0)


def _lambda(lam_ref, lam_init):
    lq = lam_ref[...]
    t1 = jnp.sum(lq[0:1, :] * lq[1:2, :], axis=-1, keepdims=True)
    t2 = jnp.sum(lq[2:3, :] * lq[3:4, :], axis=-1, keepdims=True)
    return jnp.exp(t1) - jnp.exp(t2) + lam_init


def _finish_heads(o1, o2, lam, g_ref, lam_init):
    o = o1 - lam * o2
    ms = jnp.mean(o * o, axis=-1, keepdims=True)
    return o * lax.rsqrt(ms + LN_EPS) * g_ref[...] * (1.0 - lam_init)


def _scores(qs, kblk):
    return lax.dot_general(qs, kblk, (((1,), (1,)), ((), ())), preferred_element_type=F32)


def _attn_prompt_kernel(q_ref, k_ref, vt_ref, lam_ref, g_ref, o_ref, m_ref, l_ref, acc_ref,
                        *, lam_init, tq):
    qi = pl.program_id(2)
    qs = _stack_maps(q_ref[...])
    m_ref[...] = jnp.full(m_ref.shape, NEG_BIG, F32)
    l_ref[...] = jnp.zeros(l_ref.shape, F32)
    acc_ref[...] = jnp.zeros(acc_ref.shape, F32)

    def block(j, masked):
        kblk = k_ref[pl.ds(pl.multiple_of(j * tq, tq), tq), :]
        s = _scores(kblk, qs)
        if masked:
            c = lax.broadcasted_iota(jnp.int32, s.shape, 0)
            r = lax.broadcasted_iota(jnp.int32, s.shape, 1)
            shift = CHUNK.bit_length() - 1
            qc = lax.shift_right_logical(jnp.where(r >= tq, r - tq, r), shift)
            s = jnp.where(lax.shift_right_logical(c, shift) <= qc, s, NEG_BIG)
        m_prev = m_ref[...]
        m_new = jnp.maximum(m_prev, jnp.max(s, axis=0, keepdims=True))
        alpha = jnp.exp(m_prev - m_new)
        p = jnp.exp(s - m_new)
        l_ref[...] = alpha * l_ref[...] + jnp.sum(p, axis=0, keepdims=True)
        acc_ref[...] = alpha * acc_ref[...] + jnp.dot(vt_ref[j], p.astype(BF16),
                                                      preferred_element_type=F32)
        m_ref[...] = m_new

    def body(j, carry):
        block(j, False)
        return carry

    lax.fori_loop(0, qi, body, 0)
    block(qi, True)

    o = acc_ref[...] / l_ref[...]
    lam = _lambda(lam_ref, lam_init)
    o = o[:, :tq] - lam * o[:, tq:]
    ms = jnp.mean(o * o, axis=0, keepdims=True)
    o = o * lax.rsqrt(ms + LN_EPS) * g_ref[...] * (1.0 - lam_init)
    o_ref[...] = o.T.astype(o_ref.dtype)


def _attn_prompt(q, k, vt, lambda_qk, subln_g, layer, lam_init, tq):
    b, s, _ = q.shape
    nblk = s // tq
    return pl.pallas_call(
        functools.partial(_attn_prompt_kernel, lam_init=lam_init, tq=tq),
        out_shape=jax.ShapeDtypeStruct((b, s, ATTN_WIDTH), BF16),
        grid=(b, N_HEADS, nblk),
        in_specs=[
            pl.BlockSpec((None, tq, V_DIM), lambda bi, h, qi: (bi, qi, h)),
            pl.BlockSpec((None, s, V_DIM), lambda bi, h, qi: (bi, 0, h)),
            pl.BlockSpec((None, nblk, None, V_DIM, tq), lambda bi, h, qi: (bi, 0, h, 0, 0)),
            pl.BlockSpec((None, 4, HEAD_DIM), lambda bi, h, qi: (layer, 0, 0)),
            pl.BlockSpec((None, V_DIM, 1), lambda bi, h, qi: (layer, 0, 0)),
        ],
        out_specs=pl.BlockSpec((None, tq, V_DIM), lambda bi, h, qi: (bi, qi, h)),
        scratch_shapes=[
            pltpu.VMEM((1, 2 * tq), F32),
            pltpu.VMEM((1, 2 * tq), F32),
            pltpu.VMEM((V_DIM, 2 * tq), F32),
        ],
        compiler_params=_cparams(("arbitrary", "arbitrary", "arbitrary")),
        name="attn_prompt",
    )(q, k, vt, lambda_qk, subln_g.reshape(subln_g.shape[0], V_DIM, 1))


def _attn_sample_kernel(q_ref, kn_ref, vn_ref, ck_ref, cv_ref, lam_ref, g_ref, o_ref, *, lam_init):
    tq = q_ref.shape[0]
    qs = _stack_maps(q_ref[...])
    s_p = _scores(qs, ck_ref[...].astype(BF16))
    s_n = _scores(qs, kn_ref[...])
    m = jnp.maximum(jnp.max(s_p, axis=-1, keepdims=True), jnp.max(s_n, axis=-1, keepdims=True))
    p_p = jnp.exp(s_p - m)
    p_n = jnp.exp(s_n - m)
    l = jnp.sum(p_p, axis=-1, keepdims=True) + jnp.sum(p_n, axis=-1, keepdims=True)
    acc = jnp.dot(p_p.astype(BF16), cv_ref[...].astype(BF16), preferred_element_type=F32)
    acc = acc + jnp.dot(p_n.astype(BF16), vn_ref[...], preferred_element_type=F32)
    o = acc / l
    lam = _lambda(lam_ref, lam_init)
    o_ref[...] = _finish_heads(o[:tq], o[tq:], lam, g_ref, lam_init).astype(o_ref.dtype)


def _attn_sample(q, kn, vn, cache_k, cache_v, lambda_qk, subln_g, layer, lam_init):
    b, t, _ = q.shape
    past = cache_k.shape[2]
    new = pl.BlockSpec((None, t, V_DIM), lambda bi, h: (bi, 0, h))
    old = pl.BlockSpec((None, None, past, V_DIM), lambda bi, h: (layer, bi, 0, h))
    return pl.pallas_call(
        functools.partial(_attn_sample_kernel, lam_init=lam_init),
        out_shape=jax.ShapeDtypeStruct((b, t, ATTN_WIDTH), BF16),
        grid=(b, N_HEADS),
        in_specs=[
            new, new, new, old, old,
            pl.BlockSpec((None, 4, HEAD_DIM), lambda bi, h: (layer, 0, 0)),
            pl.BlockSpec((None, 1, V_DIM), lambda bi, h: (layer, 0, 0)),
        ],
        out_specs=new,
        compiler_params=_cparams(("arbitrary", "arbitrary")),
        name="attn_sample",
    )(q, kn, vn, cache_k, cache_v, lambda_qk, subln_g.reshape(subln_g.shape[0], 1, V_DIM))


def _route(h2, wr_ref, br_ref):
    logits = jnp.dot(h2, wr_ref[...], preferred_element_type=F32, precision=lax.Precision.HIGHEST)
    mx = jnp.max(logits, axis=-1, keepdims=True)
    ex = jnp.exp(logits - mx)
    scores = ex / jnp.sum(ex, axis=-1, keepdims=True)
    sel = scores + br_ref[...]
    lane_i = lax.broadcasted_iota(jnp.int32, sel.shape, 1)
    group = lax.shift_right_logical(lane_i, EXPERTS_PER_GROUP.bit_length() - 1).astype(F32)
    lane = lane_i.astype(F32)
    neg = jnp.full_like(sel, -jnp.inf)
    none = float(N_EXPERTS)

    def top2(mask):
        m1 = jnp.max(jnp.where(mask, sel, neg), axis=-1, keepdims=True)
        i1 = jnp.min(jnp.where(mask & (sel == m1), lane, none), axis=-1, keepdims=True)
        rest = mask & (lane != i1)
        m2 = jnp.max(jnp.where(rest, sel, neg), axis=-1, keepdims=True)
        i2 = jnp.min(jnp.where(rest & (sel == m2), lane, none), axis=-1, keepdims=True)
        return m1, i1, m2, i2

    best_score = None
    best_group = None
    for g in range(N_GROUPS):
        m1, _, m2, _ = top2(group == float(g))
        gs = m1 + m2
        if g == 0:
            best_score, best_group = gs, jnp.zeros_like(gs)
        else:
            better = gs > best_score
            best_score = jnp.where(better, gs, best_score)
            best_group = jnp.where(better, float(g), best_group)
    _, i1, _, i2 = top2(group == best_group)
    chosen = (lane == i1) | (lane == i2)
    w = jnp.where(chosen, scores, 0.0)
    return w / jnp.sum(w, axis=-1, keepdims=True)


def _tail_kernel(glu_ref, halo_ref, o_ref, x_ref, mod_ref, cw_ref, cb_ref, cg_ref, cbeta_ref,
                 wo_ref, g1_ref, b1_ref, wr_ref, br_ref,
                 x1_ref, h2_ref, comb_ref, pad_ref, mixin_ref, *, alpha, zero_first_halo,
                 tiles_per_seq):
    tm = glu_ref.shape[0]
    halo = halo_ref[...]
    if zero_first_halo:
        first = (pl.program_id(0) % tiles_per_seq) == 0
        halo = jnp.where(first, jnp.zeros_like(halo), halo)
    pad_ref[0:HALO, :] = halo
    pad_ref[HALO:, :] = glu_ref[...]
    off = HALO - CONV_STATE
    rows = 64
    for r0 in range(0, tm, rows):
        acc = pad_ref[r0 + off:r0 + off + rows, :] * cw_ref[0:1, :]
        for j in range(1, CONV_WIDTH):
            acc = acc + pad_ref[r0 + off + j:r0 + off + j + rows, :] * cw_ref[j:j + 1, :]
        cv = _silu(_layer_norm(acc + cb_ref[...], cg_ref[...], cbeta_ref[...]))
        mixin_ref[r0:r0 + rows, ATTN_WIDTH:] = cv.astype(BF16)
    mixin_ref[:, :ATTN_WIDTH] = o_ref[...]
    mix = jnp.dot(mixin_ref[...], wo_ref[...], preferred_element_type=F32)
    y = alpha * x_ref[...] + mod_ref[2:3, :] * mix
    x1 = _layer_norm(y, g1_ref[...], b1_ref[...])
    x1_ref[...] = x1
    h2 = x1 * (1.0 + mod_ref[4:5, :]) + mod_ref[3:4, :]
    h2_ref[...] = h2.astype(BF16)
    comb_ref[...] = _route(h2, wr_ref, br_ref)


def _mixer_tail(glu, halo_src, o, x, mod, conv_w, conv_b, conv_ln_g, conv_ln_b, w_out_bf,
                ln1_g, ln1_b, w_router, b_router, layer, seq, tm, alpha, halo_is_glu):
    t, d = x.shape
    conv_ch = glu.shape[1]
    tiles_per_seq = seq // tm
    depth = conv_w.shape[0]
    row = lambda i: (i, 0)
    vec = lambda n: pl.BlockSpec((None, 1, n), lambda i: (layer, 0, 0))
    if halo_is_glu:
        per = tm // HALO
        halo_spec = pl.BlockSpec((HALO, conv_ch), lambda i: (jnp.maximum(i * per - 1, 0), 0))
    else:
        halo_spec = pl.BlockSpec((None, HALO, conv_ch), lambda i: (i, 0, 0))
    return pl.pallas_call(
        functools.partial(_tail_kernel, alpha=alpha, zero_first_halo=halo_is_glu,
                          tiles_per_seq=tiles_per_seq),
        out_shape=(
            jax.ShapeDtypeStruct((t, d), F32),
            jax.ShapeDtypeStruct((t, d), BF16),
            jax.ShapeDtypeStruct((t, N_EXPERTS), F32),
        ),
        grid=(t // tm,),
        in_specs=[
            pl.BlockSpec((tm, conv_ch), row),
            halo_spec,
            pl.BlockSpec((tm, ATTN_WIDTH), row),
            pl.BlockSpec((tm, d), row),
            pl.BlockSpec((None, 6, d), lambda i: (i // tiles_per_seq, 0, 0)),
            pl.BlockSpec((None, CONV_WIDTH, conv_ch), lambda i: (layer, 0, 0)),
            vec(conv_ch), vec(conv_ch), vec(conv_ch),
            pl.BlockSpec((None, d, d), lambda i: (layer, 0, 0)),
            vec(d), vec(d),
            pl.BlockSpec((d, N_EXPERTS), lambda i: (0, 0)),
            pl.BlockSpec((1, N_EXPERTS), lambda i: (0, 0)),
        ],
        out_specs=(
            pl.BlockSpec((tm, d), row),
            pl.BlockSpec((tm, d), row),
            pl.BlockSpec((tm, N_EXPERTS), row),
        ),
        scratch_shapes=[pltpu.VMEM((tm + HALO, conv_ch), F32), pltpu.VMEM((tm, d), BF16)],
        compiler_params=_cparams(("arbitrary",)),
        name="mixer_tail",
    )(glu, halo_src, o, x, mod, conv_w,
      conv_b.reshape(depth, 1, conv_ch), conv_ln_g.reshape(depth, 1, conv_ch),
      conv_ln_b.reshape(depth, 1, conv_ch), w_out_bf,
      ln1_g.reshape(depth, 1, d), ln1_b.reshape(depth, 1, d),
      w_router, b_router.reshape(1, N_EXPERTS))


def _moe_kernel(h_ref, comb_ref, x_ref, mod_ref, wg_ref, wu_ref, wd_ref, g2_ref, b2_ref,
                o_ref, acc_ref, *, alpha, nseg):
    e = pl.program_id(1)

    @pl.when(e == 0)
    def _():
        acc_ref[...] = jnp.zeros(acc_ref.shape, F32)

    h = h_ref[...]
    comb = comb_ref[...]
    lane = lax.broadcasted_iota(jnp.int32, comb.shape, 1)
    c = jnp.sum(jnp.where(lane == e, comb, 0.0), axis=-1, keepdims=True)
    hg = jnp.dot(h, wg_ref[...], preferred_element_type=F32)
    hu = jnp.dot(h, wu_ref[...], preferred_element_type=F32)
    act = (_silu(hg) * hu * c).astype(BF16)
    acc_ref[...] += jnp.dot(act, wd_ref[...], preferred_element_type=F32)

    @pl.when(e == N_EXPERTS - 1)
    def _():
        tm, d = acc_ref.shape
        seg = tm // nseg
        gate = jnp.concatenate(
            [jnp.broadcast_to(mod_ref[s, 5:6, :], (seg, d)) for s in range(nseg)], axis=0)
        y = alpha * x_ref[...] + gate * acc_ref[...]
        o_ref[...] = _layer_norm(y, g2_ref[...], b2_ref[...])


def _moe(h2, comb, x1, mod, wg_bf, wu_bf, wd_bf, ln2_g, ln2_b, layer, seq, tm, alpha):
    t, d = x1.shape
    dexp = wg_bf.shape[3]
    depth = wg_bf.shape[0]
    nseg = max(tm // seq, 1)
    tiles_per_seq = max(seq // tm, 1)
    row = lambda i, e: (i, 0)
    vec = pl.BlockSpec((None, 1, d), lambda i, e: (layer, 0, 0))
    return pl.pallas_call(
        functools.partial(_moe_kernel, alpha=alpha, nseg=nseg),
        out_shape=jax.ShapeDtypeStruct((t, d), F32),
        grid=(t // tm, N_EXPERTS),
        in_specs=[
            pl.BlockSpec((tm, d), row),
            pl.BlockSpec((tm, N_EXPERTS), row),
            pl.BlockSpec((tm, d), row),
            pl.BlockSpec((nseg, 6, d), lambda i, e: (i // tiles_per_seq, 0, 0)),
            pl.BlockSpec((None, None, d, dexp), lambda i, e: (layer, e, 0, 0)),
            pl.BlockSpec((None, None, d, dexp), lambda i, e: (layer, e, 0, 0)),
            pl.BlockSpec((None, None, dexp, d), lambda i, e: (layer, e, 0, 0)),
            vec, vec,
        ],
        out_specs=pl.BlockSpec((tm, d), row),
        scratch_shapes=[pltpu.VMEM((tm, d), F32)],
        compiler_params=_cparams(("arbitrary", "arbitrary")),
        name="moe",
    )(h2, comb, x1, mod, wg_bf, wu_bf, wd_bf,
      ln2_g.reshape(depth, 1, d), ln2_b.reshape(depth, 1, d))


def _pick_tile(n, want):
    t = min(n, want)
    while n % t:
        t //= 2
    return t


def kernel(x_prompt, x_sample, cache_k, cache_v, state_conv, c_prompt, c_sample, w_ada, b_ada, w_in, lambda_qk, subln_g, conv_w, conv_b, conv_ln_g, conv_ln_b, w_out, ln1_g, ln1_b, ln2_g, ln2_b, w_router, b_router, w_gate, w_up, w_down):
    bp, sp, d = x_prompt.shape
    bs, ss, _ = x_sample.shape
    depth = w_ada.shape[0]
    past = cache_k.shape[2]
    conv_ch = d - ATTN_WIDTH
    alpha = (2 * depth) ** 0.25

    mod = _ada_all(jnp.concatenate([c_prompt, c_sample], axis=0), w_ada, b_ada)
    mod = mod.reshape(depth, bp + bs, 6, d)

    w_in_bf = w_in.astype(BF16)
    w_out_bf = w_out.astype(BF16)
    wg_bf, wu_bf, wd_bf = w_gate.astype(BF16), w_up.astype(BF16), w_down.astype(BF16)

    tab_p = _rope_tables(jnp.arange(sp, dtype=jnp.int32))
    tab_s = _rope_tables(past + jnp.arange(ss, dtype=jnp.int32))
    ck = cache_k.reshape(depth, bs, past, QK_COLS)
    cv = cache_v.reshape(depth, bs, past, ATTN_WIDTH)
    hist = jnp.concatenate(
        [jnp.zeros((depth, bs, HALO - CONV_STATE, conv_ch), F32), state_conv], axis=2)

    tm_p = _pick_tile(sp, 512)
    tq = tm_p
    tm_moe_p = _pick_tile(sp, 1024)

    xp = x_prompt.reshape(bp * sp, d)
    xs = x_sample.reshape(bs * ss, d)
    outs = {n: [] for n in ("kp", "vp", "cp", "ks", "vs", "cs")}
    for l in range(depth):
        lam_init = 0.8 - 0.6 * math.exp(-0.3 * l)
        mod_p, mod_s = mod[l, :bp], mod[l, bp:]

        q, kb, vt, k32, v32, glu = _in_proj(xp, mod_p, w_in_bf, tab_p, l, sp, tm_p, True)
        o = _attn_prompt(q.reshape(bp, sp, QK_COLS), kb.reshape(bp, sp, QK_COLS),
                         vt.reshape(bp, sp // tq, N_HEADS, V_DIM, tq), lambda_qk, subln_g, l,
                         lam_init, tq)
        x1, h2, comb = _mixer_tail(glu, glu, o.reshape(bp * sp, ATTN_WIDTH), xp, mod_p,
                                   conv_w, conv_b, conv_ln_g, conv_ln_b, w_out_bf, ln1_g, ln1_b,
                                   w_router, b_router, l, sp, tm_p, alpha, True)
        xp = _moe(h2, comb, x1, mod_p, wg_bf, wu_bf, wd_bf, ln2_g, ln2_b, l, sp, tm_moe_p, alpha)
        outs["kp"].append(k32.reshape(bp, sp, N_HEADS, 2, HEAD_DIM))
        outs["vp"].append(v32.reshape(bp, sp, N_HEADS, V_DIM))
        outs["cp"].append(glu.reshape(bp, sp, conv_ch)[:, sp - CONV_STATE:])

        q, kb, vb, k32, v32, glu = _in_proj(xs, mod_s, w_in_bf, tab_s, l, ss, ss, False)
        o = _attn_sample(q.reshape(bs, ss, QK_COLS), kb.reshape(bs, ss, QK_COLS),
                         vb.reshape(bs, ss, ATTN_WIDTH), ck, cv, lambda_qk, subln_g, l, lam_init)
        x1, h2, comb = _mixer_tail(glu, hist[l], o.reshape(bs * ss, ATTN_WIDTH), xs, mod_s,
                                   conv_w, conv_b, conv_ln_g, conv_ln_b, w_out_bf, ln1_g, ln1_b,
                                   w_router, b_router, l, ss, ss, alpha, False)
        xs = _moe(h2, comb, x1, mod_s, wg_bf, wu_bf, wd_bf, ln2_g, ln2_b, l, ss, bs * ss, alpha)
        outs["ks"].append(k32.reshape(bs, ss, N_HEADS, 2, HEAD_DIM))
        outs["vs"].append(v32.reshape(bs, ss, N_HEADS, V_DIM))
        new_conv = jnp.concatenate([state_conv[l], glu.reshape(bs, ss, conv_ch)], axis=1)
        outs["cs"].append(new_conv[:, -CONV_STATE:])

    return (xp.reshape(bp, sp, d), xs.reshape(bs, ss, d),
            jnp.stack(outs["kp"]), jnp.stack(outs["vp"]), jnp.stack(outs["cp"]),
            jnp.stack(outs["ks"]), jnp.stack(outs["vs"]), jnp.stack(outs["cs"]))
```
